```python
import math
import jax, jax.numpy as jnp
from jax import lax
import numpy as np

D_MODEL = 1024
BATCH = 4
SEQ = 8192
DEPTH = 1

CHUNK = 64
Q_BLOCK = 2 * CHUNK
N_META = 16
HEAD_DIM = 64
SB_HEADS = 8
SB_WIDTH = SB_HEADS * HEAD_DIM
LRU_WIDTH = D_MODEL - SB_WIDTH
LRU_BLOCKS = 8
LRU_BLOCK_DIM = LRU_WIDTH // LRU_BLOCKS
LRU_C = 8.0
LRU_CONV = 4
D_FF = 2816
FFN_CONV = 3
IN_COLS = 3 * SB_WIDTH + 2 * LRU_WIDTH
EPS = 1e-6

kernel_name = "hymba_stickbreak_rglru_convffn"


def rmsnorm(x, g):
    xf = x.astype(jnp.float32)
    y = xf * lax.rsqrt(jnp.mean(xf * xf, axis=-1, keepdims=True) + EPS)
    return (y * g.astype(jnp.float32)).astype(x.dtype)


def causal_dwconv(x, w, b):
    k_w = w.shape[0]
    c = x.shape[-1]
    out = lax.conv_general_dilated(
        x, w[:, None, :].astype(x.dtype), window_strides=(1,), padding=[(k_w - 1, 0)],
        dimension_numbers=("NWC", "WIO", "NWC"), feature_group_count=c)
    return out + b.astype(x.dtype)


def stick_breaking(q, k, v):
    t_len = q.shape[2]
    scale = HEAD_DIM ** -0.5
    outs = []
    for blk in range(t_len // Q_BLOCK):
        q0 = blk * Q_BLOCK
        k_end = q0 + Q_BLOCK
        qb = q[:, :, q0:k_end]
        kb = k[:, :, :k_end]
        vb = v[:, :, :k_end]
        z = jnp.einsum("bhqd,bhkd->bhqk", qb, kb) * scale
        t_pos = q0 + jnp.arange(Q_BLOCK)[:, None]
        s_pos = jnp.arange(k_end)[None, :]
        mask = s_pos < t_pos
        log_keep = jnp.where(mask, jax.nn.log_sigmoid(-z), 0.0)
        after = lax.cumsum(log_keep, axis=3, reverse=True) - log_keep
        w = jnp.where(mask, jnp.exp(jax.nn.log_sigmoid(z) + after), 0.0)
        outs.append(jnp.einsum("bhqk,bhkd->bhqd", w, vb))
    return jnp.concatenate(outs, axis=2)


def rg_lru(xr, r, i, lam):
    log_a = -LRU_C * r * jax.nn.softplus(-lam.astype(jnp.float32))
    a = jnp.exp(log_a)
    b = jnp.sqrt(-jnp.expm1(2.0 * log_a)) * (i * xr)

    def combine(left, right):
        a1, b1 = left
        a2, b2 = right
        return a1 * a2, a2 * b1 + b2

    _, h = lax.associative_scan(combine, (a, b), axis=1)
    return h


def setup_inputs(seed: int = 0) -> dict:
    key = jax.random.key(seed)
    ks = jax.random.split(key, 24)
    f32 = jnp.float32

    def nrm(k, shape, scale):
        return jax.random.normal(k, shape, f32) * scale

    a8 = jax.random.uniform(ks[12], (DEPTH, LRU_WIDTH), f32, 0.9, 0.999)
    a_base = a8 ** (1.0 / LRU_C)
    lru_lambda = jnp.log(a_base) - jnp.log1p(-a_base)

    return {
        "x": nrm(ks[0], (BATCH, SEQ, D_MODEL), 1.0),
        "meta_tokens": nrm(ks[1], (N_META, D_MODEL), 1.0),
        "norm1_g": 1.0 + nrm(ks[2], (DEPTH, D_MODEL), 0.01),
        "w_in": nrm(ks[3], (DEPTH, D_MODEL, IN_COLS), D_MODEL ** -0.5),
        "q_norm_g": 1.0 + nrm(ks[4], (DEPTH, HEAD_DIM), 0.01),
        "k_norm_g": 1.0 + nrm(ks[5], (DEPTH, HEAD_DIM), 0.01),
        "conv_w": nrm(ks[6], (DEPTH, LRU_CONV, LRU_WIDTH), LRU_CONV ** -0.5),
        "conv_b": nrm(ks[7], (DEPTH, LRU_WIDTH), 0.01),
        "w_rg_a": nrm(ks[8], (DEPTH, LRU_BLOCKS, LRU_BLOCK_DIM, LRU_BLOCK_DIM), LRU_BLOCK_DIM ** -0.5),
        "b_rg_a": nrm(ks[9], (DEPTH, LRU_WIDTH), 0.01),
        "w_rg_i": nrm(ks[10], (DEPTH, LRU_BLOCKS, LRU_BLOCK_DIM, LRU_BLOCK_DIM), LRU_BLOCK_DIM ** -0.5),
        "b_rg_i": nrm(ks[11], (DEPTH, LRU_WIDTH), 0.01),
        "lru_lambda": lru_lambda,
        "w_out": nrm(ks[13], (DEPTH, D_MODEL, D_MODEL), D_MODEL ** -0.5),
        "norm2_g": 1.0 + nrm(ks[14], (DEPTH, D_MODEL), 0.01),
        "w_ffn_in": nrm(ks[15], (DEPTH, D_MODEL, 2 * D_FF), D_MODEL ** -0.5),
        "ffn_conv_w": nrm(ks[16], (DEPTH, FFN_CONV, 2 * D_FF), FFN_CONV ** -0.5),
        "ffn_conv_b": nrm(ks[17], (DEPTH, 2 * D_FF), 0.01),
        "w_ffn_out": nrm(ks[18], (DEPTH, D_FF, D_MODEL), D_FF ** -0.5),
    }


def reference(x, meta_tokens, norm1_g, w_in, q_norm_g, k_norm_g, conv_w, conv_b,
              w_rg_a, b_rg_a, w_rg_i, b_rg_i, lru_lambda, w_out, norm2_g,
              w_ffn_in, ffn_conv_w, ffn_conv_b, w_ffn_out):
    bsz, seq, d = x.shape
    total = seq + N_META
    t_pad = -(-total // Q_BLOCK) * Q_BLOCK
    meta = jnp.broadcast_to(meta_tokens[None].astype(x.dtype), (bsz, N_META, d))
    pad = jnp.zeros((bsz, t_pad - total, d), x.dtype)
    h = jnp.concatenate([meta, x, pad], axis=1)
    t_len = t_pad

    def to_heads(t):
        return t.reshape(bsz, t_len, SB_HEADS, HEAD_DIM).transpose(0, 2, 1, 3)

    for layer in range(DEPTH):
        hn = rmsnorm(h, norm1_g[layer])
        proj = hn @ w_in[layer]
        q, k, v, xr, yg = jnp.split(
            proj, [SB_WIDTH, 2 * SB_WIDTH, 3 * SB_WIDTH, 3 * SB_WIDTH + LRU_WIDTH], axis=-1)

        qh = rmsnorm(to_heads(q), q_norm_g[layer]).astype(jnp.float32)
        kh = rmsnorm(to_heads(k), k_norm_g[layer]).astype(jnp.float32)
        vh = to_heads(v).astype(jnp.float32)
        o_sb = stick_breaking(qh, kh, vh).transpose(0, 2, 1, 3).reshape(bsz, t_len, SB_WIDTH)

        xc = causal_dwconv(xr, conv_w[layer], conv_b[layer]).astype(jnp.float32)
        xblk = xc.reshape(bsz, t_len, LRU_BLOCKS, LRU_BLOCK_DIM)
        r_gate = jax.nn.sigmoid(
            jnp.einsum("btnc,ncd->btnd", xblk, w_rg_a[layer].astype(jnp.float32)).reshape(bsz, t_len, LRU_WIDTH)
            + b_rg_a[layer].astype(jnp.float32))
        i_gate = jax.nn.sigmoid(
            jnp.einsum("btnc,ncd->btnd", xblk, w_rg_i[layer].astype(jnp.float32)).reshape(bsz, t_len, LRU_WIDTH)
            + b_rg_i[layer].astype(jnp.float32))
        h_lru = rg_lru(xc, r_gate, i_gate, lru_lambda[layer])
        o_lru = jax.nn.gelu(yg.astype(jnp.float32), approximate=True) * h_lru

        mixed = jnp.concatenate([o_sb, o_lru], axis=-1).astype(h.dtype)
        h = h + mixed @ w_out[layer]

        hn2 = rmsnorm(h, norm2_g[layer])
        ug = causal_dwconv(hn2 @ w_ffn_in[layer], ffn_conv_w[layer], ffn_conv_b[layer])
        u, g = jnp.split(ug, [D_FF], axis=-1)
        h = h + (jax.nn.silu(g) * u) @ w_ffn_out[layer]

    return h[:, N_META:N_META + seq]
```

```python
import functools
import math

import jax
import jax.numpy as jnp
from jax import lax
from jax.experimental import pallas as pl
from jax.experimental.pallas import tpu as pltpu

N_META = 16
Q_BLOCK = 128
HEAD_DIM = 64
LRU_C = 8.0
EPS = 1e-6

LANES = 128
SUBLANES = 8
VMEM_LIMIT_BYTES = 56 * 1024 * 1024
LOG2E = math.log2(math.e)
ZERO_WEIGHT_LOG2 = -160.0

F32 = jnp.float32
BF16 = jnp.bfloat16


def _pick_block(n, target):
    best = Q_BLOCK
    for m in range(Q_BLOCK, min(n, target) + 1, Q_BLOCK):
        if n % m == 0:
            best = m
    return best


def _split_hi_lo(x):
    hi = x.astype(BF16)
    lo = (x - hi.astype(F32)).astype(BF16)
    return hi, lo


def _const_spec(shape):
    return pl.BlockSpec(shape, lambda *_: (0,) * len(shape), pipeline_mode=pl.Buffered(1))


def _inproj_kernel(h_ref, g1_ref, w_ref, gq_ref, gk_ref, grp_ref,
                   q_ref, k_ref, v_ref, xr_ref, yg_ref):
    h = h_ref[...]
    ms = jnp.mean(h * h, axis=-1, keepdims=True)
    hn = (h * lax.rsqrt(ms + EPS) * g1_ref[...]).astype(BF16)
    proj = jnp.dot(hn, w_ref[...], preferred_element_type=F32)
    sw = q_ref.shape[-1]
    lw = xr_ref.shape[-1]
    grp = grp_ref[...]

    def head_norm(t, g):
        hi, lo = _split_hi_lo(t * t)
        hms = (jnp.dot(hi, grp, preferred_element_type=F32)
               + jnp.dot(lo, grp, preferred_element_type=F32))
        return t * lax.rsqrt(hms + EPS) * g

    q_ref[...] = head_norm(proj[:, :sw], gq_ref[...] * (HEAD_DIM ** -0.5 * LOG2E)).astype(BF16)
    k_ref[...] = head_norm(proj[:, sw:2 * sw], gk_ref[...]).astype(BF16)
    v_ref[...] = proj[:, 2 * sw:3 * sw].astype(BF16)
    xr_ref[...] = proj[:, 3 * sw:3 * sw + lw]
    yg_ref[...] = proj[:, 3 * sw + lw:]


def _inproj(h, g1, w, gq, gk, grp, bm):
    bsz, t_len, d = h.shape
    sw = grp.shape[0]
    lw = (w.shape[1] - 3 * sw) // 2
    row = lambda width: pl.BlockSpec((None, bm, width), lambda b, i: (b, i, 0))
    return pl.pallas_call(
        _inproj_kernel,
        grid=(bsz, t_len // bm),
        in_specs=[row(d), _const_spec(g1.shape), _const_spec(w.shape),
                  _const_spec(gq.shape), _const_spec(gk.shape), _const_spec(grp.shape)],
        out_specs=[row(sw), row(sw), row(sw), row(lw), row(lw)],
        out_shape=[jax.ShapeDtypeStruct((bsz, t_len, sw), BF16)] * 3
                  + [jax.ShapeDtypeStruct((bsz, t_len, lw), F32)] * 2,
        compiler_params=pltpu.CompilerParams(
            dimension_semantics=("arbitrary", "arbitrary"),
            vmem_limit_bytes=VMEM_LIMIT_BYTES),
        name="inproj",
    )(h, g1, w, gq, gk, grp)


def _shift_rows(x, d, fill):
    n = x.shape[0]
    if d % SUBLANES == 0:
        return jnp.concatenate([jnp.full((d, x.shape[1]), fill, x.dtype), x[:n - d]], axis=0)
    rows = lax.broadcasted_iota(jnp.int32, x.shape, 0)
    return jnp.where(rows < d, fill, pltpu.roll(x, d, 0))


def _shift_rows_tail(x, d, tail):
    r = pltpu.roll(x, d, 0)
    rt = pltpu.roll(tail, d, 0)
    rows = lax.broadcasted_iota(jnp.int32, tail.shape, 0)
    first = jnp.where(rows < d, rt, r[:SUBLANES])
    return jnp.concatenate([first, r[SUBLANES:]], axis=0)


def _lru_kernel(xr_ref, yg_ref, cw_ref, cb_ref, wg_ref, ba_ref, bi_ref, lam_ref,
                o_ref, tail_ref, carry_ref):
    @pl.when(pl.program_id(1) == 0)
    def _():
        tail_ref[...] = jnp.zeros_like(tail_ref)
        carry_ref[...] = jnp.zeros_like(carry_ref)

    xr = xr_ref[...]
    bt, c = xr.shape
    tail = tail_ref[...]
    cw = cw_ref[...]
    taps = cw.shape[0]
    xc = cw[taps - 1:taps] * xr + cb_ref[...]
    for d in range(1, taps):
        xc = xc + cw[taps - 1 - d:taps - d] * _shift_rows_tail(xr, d, tail)
    tail_ref[...] = xr[bt - SUBLANES:]

    gates = jnp.dot(xc.astype(BF16), wg_ref[...], preferred_element_type=F32)
    r_gate = jax.nn.sigmoid(gates[:, :c] + ba_ref[...])
    i_gate = jax.nn.sigmoid(gates[:, c:] + bi_ref[...])
    lam = lam_ref[...]
    neg_softplus = jnp.maximum(-lam, 0.0) + jnp.log(1.0 + jnp.exp(-jnp.abs(lam)))
    log_a = (-LRU_C) * r_gate * neg_softplus
    a = jnp.exp(log_a)
    b = jnp.sqrt((1.0 - a) * (1.0 + a)) * (i_gate * xc)

    d = 1
    while d < bt:
        b = a * _shift_rows(b, d, 0.0) + b
        a = a * _shift_rows(a, d, 1.0)
        d *= 2
    hs = a * carry_ref[...] + b
    carry_ref[...] = hs[bt - 1:bt]
    o_ref[...] = (jax.nn.gelu(yg_ref[...], approximate=True) * hs).astype(BF16)


def _lru(xr, yg, cw, cb, wg, ba, bi, lam, bt):
    bsz, t_len, c = xr.shape
    row = pl.BlockSpec((None, bt, c), lambda b, i: (b, i, 0))
    return pl.pallas_call(
        _lru_kernel,
        grid=(bsz, t_len // bt),
        in_specs=[row, row] + [_const_spec(a.shape) for a in (cw, cb, wg, ba, bi, lam)],
        out_specs=row,
        out_shape=jax.ShapeDtypeStruct((bsz, t_len, c), BF16),
        scratch_shapes=[pltpu.VMEM((SUBLANES, c), F32), pltpu.VMEM((1, c), F32)],
        compiler_params=pltpu.CompilerParams(
            dimension_semantics=("arbitrary", "arbitrary"),
            vmem_limit_bytes=VMEM_LIMIT_BYTES),
        name="rglru",
    )(xr, yg, cw, cb, wg, ba, bi, lam)


def _attn_kernel(q_ref, k_ref, v_ref, uu_ref, o_ref, r_ref, acc_ref, rmax_ref):
    i = pl.program_id(1)
    bq = q_ref.shape[0]
    bk = bq
    n_pairs = q_ref.shape[1] // LANES
    word_lane = lax.broadcasted_iota(jnp.int32, (bk // 2, LANES), 1)
    first_head_bits = jnp.where(word_lane < HEAD_DIM, -1, 0)
    second_head_bits = jnp.where(word_lane < HEAD_DIM, 0, -1)
    row = lax.broadcasted_iota(jnp.int32, (bq, 2 * bk), 0)
    col = lax.broadcasted_iota(jnp.int32, (bq, 2 * bk), 1)
    causal = jnp.where(col >= bk, col - bk, col) < row
    sign_bit = jnp.int32(-2 ** 31)

    def stack_heads(ref, start, cols):
        words = pltpu.bitcast(ref[pl.ds(start, bk), cols], jnp.int32)
        return pltpu.bitcast(
            jnp.concatenate([words & first_head_bits, words & second_head_bits], axis=0), BF16)

    def sweep(blocks, fresh):
        starts = [pl.multiple_of(j * bk, bk) for j, _ in blocks]
        ys, lhs = [], []
        for (_, on_diagonal), start in zip(blocks, starts):
            for p in range(n_pairs):
                cols = slice(p * LANES, (p + 1) * LANES)
                y = lax.dot_general(q_ref[:, cols], stack_heads(k_ref, start, cols),
                                    (((1,), (1,)), ((), ())), preferred_element_type=F32)
                neg_abs = pltpu.bitcast(pltpu.bitcast(y, jnp.int32) | sign_bit, F32)
                sp = jnp.maximum(y, 0.0) + jnp.log2(1.0 + jnp.exp2(neg_abs))
                if on_diagonal:
                    sp = jnp.where(causal, sp, 0.0)
                hi, lo = _split_hi_lo(sp)
                lhs += [jnp.concatenate([hi[:, :bk], lo[:, :bk]], axis=1),
                        jnp.concatenate([hi[:, bk:], lo[:, bk:]], axis=1)]
                ys.append(y)
        ss = jnp.dot(jnp.concatenate(lhs, axis=0), uu_ref[...], preferred_element_type=F32)
        r_max = None
        for p in range(n_pairs):
            cols = slice(p * LANES, (p + 1) * LANES)
            r = None if fresh else r_ref[p]
            acc = None
            for n, ((_, on_diagonal), start) in enumerate(zip(blocks, starts)):
                base = (n * n_pairs + p) * 2 * bq
                sa = ss[base:base + bq]
                sb = ss[base + bq:base + 2 * bq]
                suffix = jnp.concatenate([sa[:, :bk], sb[:, :bk]], axis=1)
                total = jnp.concatenate([sa[:, bk:], sb[:, bk:]], axis=1)
                log2_w = ys[n * n_pairs + p] + suffix
                if r is not None:
                    log2_w = log2_w + r
                w = jnp.exp2(log2_w)
                if on_diagonal:
                    w = jnp.where(causal, w, 0.0)
                r = total if r is None else r + total
                part = jnp.dot(w.astype(BF16), stack_heads(v_ref, start, cols),
                               preferred_element_type=F32)
                acc = part if acc is None else acc + part
            r_ref[p] = r
            acc_ref[:, cols] = acc if fresh else acc_ref[:, cols] + acc
            r_max = r if r_max is None else jnp.maximum(r_max, r)
        return jnp.max(r_max)

    n_first = 3

    @pl.when(i >= n_first - 1)
    def _():
        rmax_ref[0] = sweep([(i - n, n == 0) for n in range(n_first)], fresh=True)

    @pl.when(i < n_first - 1)
    def _():
        rmax_ref[0] = sweep([(i, True)], fresh=True)

    lax.while_loop(lambda c: jnp.logical_and(c[0] >= 0, c[1] > ZERO_WEIGHT_LOG2),
                   lambda c: (c[0] - 1, sweep([(c[0], False)], fresh=False)),
                   (jnp.where(i >= n_first - 1, i - n_first, i - 1), rmax_ref[0]))
    o_ref[...] = acc_ref[...].astype(BF16)


def _attention(q, k, v, bq):
    bsz, t_len, sw = q.shape
    n_pairs = sw // LANES
    j = lax.broadcasted_iota(jnp.int32, (2 * bq, 2 * bq), 0) % bq
    s = lax.broadcasted_iota(jnp.int32, (2 * bq, 2 * bq), 1)
    uu = jnp.where((s >= bq) | (j >= s), -1.0, 0.0).astype(BF16)
    blk = pl.BlockSpec((None, bq, sw), lambda b, i: (b, i, 0))
    seq = pl.BlockSpec((None, t_len, sw), lambda b, i: (b, 0, 0))
    return pl.pallas_call(
        _attn_kernel,
        grid=(bsz, t_len // bq),
        in_specs=[blk, seq, seq, _const_spec(uu.shape)],
        out_specs=blk,
        out_shape=jax.ShapeDtypeStruct((bsz, t_len, sw), BF16),
        scratch_shapes=[pltpu.VMEM((n_pairs, bq, 2 * bq), F32), pltpu.VMEM((bq, sw), F32),
                        pltpu.SMEM((1,), F32)],
        compiler_params=pltpu.CompilerParams(
            dimension_semantics=("arbitrary", "arbitrary"),
            vmem_limit_bytes=VMEM_LIMIT_BYTES),
        name="stickbreak",
    )(q, k, v, uu)


def _ffn_kernel(h_ref, osb_ref, olru_ref, wo_ref, g2_ref, wi_ref, cw_ref, cb_ref, wd_ref,
                out_ref, tail_ref, *, chunk):
    @pl.when(pl.program_id(1) == 0)
    def _():
        tail_ref[...] = jnp.zeros_like(tail_ref)

    bm = h_ref.shape[0]
    sw = osb_ref.shape[1]
    dff = wd_ref.shape[0]
    taps = cw_ref.shape[0]
    h1 = h_ref[...] + (jnp.dot(osb_ref[...], wo_ref[:sw, :], preferred_element_type=F32)
                       + jnp.dot(olru_ref[...], wo_ref[sw:, :], preferred_element_type=F32))
    ms = jnp.mean(h1 * h1, axis=-1, keepdims=True)
    hn = (h1 * lax.rsqrt(ms + EPS) * g2_ref[...]).astype(BF16)

    def conv_branch(cols):
        pre = jnp.dot(hn, wi_ref[:, cols], preferred_element_type=F32)
        tail = tail_ref[:, cols]
        tail_ref[:, cols] = pre[bm - SUBLANES:, :]
        out = cw_ref[taps - 1:taps, cols] * pre + cb_ref[:, cols]
        for d in range(1, taps):
            out = out + cw_ref[taps - 1 - d:taps - d, cols] * _shift_rows_tail(pre, d, tail)
        return out

    acc = None
    for j in range(dff // chunk):
        u = conv_branch(slice(j * chunk, (j + 1) * chunk))
        g = conv_branch(slice(dff + j * chunk, dff + (j + 1) * chunk))
        act = (jax.nn.silu(g) * u).astype(BF16)
        part = jnp.dot(act, wd_ref[j * chunk:(j + 1) * chunk, :], preferred_element_type=F32)
        acc = part if acc is None else acc + part
    out_ref[...] = h1 + acc


def _ffn(h, osb, olru, wo, g2, wi, cw, cb, wd, bm, chunk):
    bsz, t_len, d = h.shape
    sw = osb.shape[-1]
    lw = olru.shape[-1]
    row = lambda width: pl.BlockSpec((None, bm, width), lambda b, i: (b, i, 0))
    return pl.pallas_call(
        functools.partial(_ffn_kernel, chunk=chunk),
        grid=(bsz, t_len // bm),
        in_specs=[row(d), row(sw), row(lw)] + [_const_spec(a.shape) for a in (wo, g2, wi, cw, cb, wd)],
        out_specs=row(d),
        out_shape=jax.ShapeDtypeStruct((bsz, t_len, d), F32),
        scratch_shapes=[pltpu.VMEM((SUBLANES, wi.shape[1]), F32)],
        compiler_params=pltpu.CompilerParams(
            dimension_semantics=("arbitrary", "arbitrary"),
            vmem_limit_bytes=VMEM_LIMIT_BYTES),
        name="outproj_convffn",
    )(h, osb, olru, wo, g2, wi, cw, cb, wd)


def _block_diag(w):
    n, c, d = w.shape
    eye = jnp.eye(n, dtype=w.dtype)
    return (eye[:, None, :, None] * w[:, :, None, :]).reshape(n * c, n * d)


def kernel(x, meta_tokens, norm1_g, w_in, q_norm_g, k_norm_g, conv_w, conv_b, w_rg_a, b_rg_a,
           w_rg_i, b_rg_i, lru_lambda, w_out, norm2_g, w_ffn_in, ffn_conv_w, ffn_conv_b, w_ffn_out):
    bsz, seq, d = x.shape
    total = seq + N_META
    t_len = -(-total // Q_BLOCK) * Q_BLOCK
    meta = jnp.broadcast_to(meta_tokens[None].astype(x.dtype), (bsz, N_META, d))
    pad = jnp.zeros((bsz, t_len - total, d), x.dtype)
    h = jnp.concatenate([meta, x, pad], axis=1)

    lw = conv_w.shape[-1]
    sw = (w_in.shape[-1] - 2 * lw) // 3
    n_heads = sw // HEAD_DIM
    bm = _pick_block(t_len, 640)
    chunk = 2 * LANES
    head_id = jnp.arange(sw) // HEAD_DIM
    grp = jnp.where(head_id[:, None] == head_id[None, :], 1.0 / HEAD_DIM, 0.0).astype(BF16)
    row2d = lambda a: a.reshape(1, -1)

    for layer in range(norm1_g.shape[0]):
        q, k, v, xr, yg = _inproj(
            h, row2d(norm1_g[layer]), w_in[layer].astype(BF16),
            row2d(jnp.tile(q_norm_g[layer], n_heads)), row2d(jnp.tile(k_norm_g[layer], n_heads)),
            grp, bm)
        w_gates = jnp.concatenate(
            [_block_diag(w_rg_a[layer]), _block_diag(w_rg_i[layer])], axis=1).astype(BF16)
        o_lru = _lru(xr, yg, conv_w[layer], row2d(conv_b[layer]), w_gates,
                     row2d(b_rg_a[layer]), row2d(b_rg_i[layer]), row2d(lru_lambda[layer]), bm)
        o_sb = _attention(q, k, v, Q_BLOCK)
        h = _ffn(h, o_sb, o_lru, w_out[layer].astype(BF16), row2d(norm2_g[layer]),
                 w_ffn_in[layer].astype(BF16), ffn_conv_w[layer], row2d(ffn_conv_b[layer]),
                 w_ffn_out[layer].astype(BF16), bm, chunk)

    return h[:, N_META:N_META + seq]
```

```python
import functools
import math

import jax
import jax.numpy as jnp
from jax import lax
from jax.experimental import pallas as pl
from jax.experimental.pallas import tpu as pltpu

N_META = 16
Q_BLOCK = 128
HEAD_DIM = 64
LRU_C = 8.0
EPS = 1e-6

LANES = 128
SUBLANES = 8
VMEM_LIMIT_BYTES = 56 * 1024 * 1024
LOG2E = math.log2(math.e)
ZERO_WEIGHT_LOG2 = -160.0

F32 = jnp.float32
BF16 = jnp.bfloat16


def _pick_block(n, target):
    best = Q_BLOCK
    for m in range(Q_BLOCK, min(n, target) + 1, Q_BLOCK):
        if n % m == 0:
            best = m
    return best


def _split_hi_lo(x):
    hi = x.astype(BF16)
    lo = (x - hi.astype(F32)).astype(BF16)
    return hi, lo


def _const_spec(shape):
    return pl.BlockSpec(shape, lambda *_: (0,) * len(shape), pipeline_mode=pl.Buffered(1))


def _shift_rows_tail(x, d, tail):
    r = pltpu.roll(x, d, 0)
    rt = pltpu.roll(tail, d, 0)
    rows = lax.broadcasted_iota(jnp.int32, tail.shape, 0)
    first = jnp.where(rows < d, rt, r[:SUBLANES])
    return jnp.concatenate([first, r[SUBLANES:]], axis=0)


def _linear_scan(a, b, carry):
    n = a.shape[0]
    sub = lax.broadcasted_iota(jnp.int32, a.shape, 0) % SUBLANES
    d = 1
    while d < SUBLANES:
        inside = sub >= d
        b = a * jnp.where(inside, pltpu.roll(b, d, 0), 0.0) + b
        a = a * jnp.where(inside, pltpu.roll(a, d, 0), 1.0)
        d *= 2
    groups = []
    for g in range(n // SUBLANES):
        rows = slice(g * SUBLANES, (g + 1) * SUBLANES)
        hg = a[rows] * carry + b[rows]
        carry = hg[SUBLANES - 1:]
        groups.append(hg)
    return jnp.concatenate(groups, axis=0)


def _inproj_kernel(h_ref, g1_ref, w_ref, gq_ref, gk_ref, grp_ref,
                   cw_ref, cb_ref, wg_ref, ba_ref, bi_ref, lam_ref,
                   q_ref, k_ref, v_ref, o_ref, tail_ref, carry_ref):
    @pl.when(pl.program_id(1) == 0)
    def _():
        tail_ref[...] = jnp.zeros_like(tail_ref)
        carry_ref[...] = jnp.zeros_like(carry_ref)

    sw = q_ref.shape[-1]
    lw = o_ref.shape[-1]
    bm = h_ref.shape[0]
    h = h_ref[...]
    ms = jnp.mean(h * h, axis=-1, keepdims=True)
    hn = (h * lax.rsqrt(ms + EPS) * g1_ref[...]).astype(BF16)

    def proj(lo, hi):
        return jnp.dot(hn, w_ref[:, lo:hi], preferred_element_type=F32)

    def head_norm(t, g):
        hms = jnp.dot((t * t).astype(BF16), grp_ref[...], preferred_element_type=F32)
        return t * lax.rsqrt(hms + EPS) * g

    xr = proj(3 * sw, 3 * sw + lw)
    yg = proj(3 * sw + lw, 3 * sw + 2 * lw)
    tail = tail_ref[...]
    taps = cw_ref.shape[0]
    xc = cw_ref[taps - 1:taps, :] * xr + cb_ref[...]
    for d in range(1, taps):
        xc = xc + cw_ref[taps - 1 - d:taps - d, :] * _shift_rows_tail(xr, d, tail)
    tail_ref[...] = xr[bm - SUBLANES:]
    gates = jnp.dot(xc.astype(BF16), wg_ref[...], preferred_element_type=F32)

    q_raw = proj(0, sw)
    k_raw = proj(sw, 2 * sw)
    v_ref[...] = proj(2 * sw, 3 * sw).astype(BF16)
    q_ref[...] = head_norm(q_raw, gq_ref[...] * (HEAD_DIM ** -0.5 * LOG2E)).astype(BF16)
    k_ref[...] = head_norm(k_raw, gk_ref[...]).astype(BF16)

    r_gate = jax.nn.sigmoid(gates[:, :lw] + ba_ref[...])
    i_gate = jax.nn.sigmoid(gates[:, lw:] + bi_ref[...])
    lam = lam_ref[...]
    neg_softplus = jnp.maximum(-lam, 0.0) + jnp.log(1.0 + jnp.exp(-jnp.abs(lam)))
    a = jnp.exp((-LRU_C) * r_gate * neg_softplus)
    b = jnp.sqrt((1.0 - a) * (1.0 + a)) * (i_gate * xc)
    hs = _linear_scan(a, b, carry_ref[...])
    carry_ref[...] = hs[bm - 1:]
    o_ref[...] = (jax.nn.gelu(yg, approximate=True) * hs).astype(BF16)


def _inproj(h, g1, w, gq, gk, grp, cw, cb, wg, ba, bi, lam, bm):
    bsz, t_len, d = h.shape
    sw = grp.shape[0]
    lw = cw.shape[1]
    row = lambda width: pl.BlockSpec((None, bm, width), lambda b, i: (b, i, 0))
    consts = (g1, w, gq, gk, grp, cw, cb, wg, ba, bi, lam)
    return pl.pallas_call(
        _inproj_kernel,
        grid=(bsz, t_len // bm),
        in_specs=[row(d)] + [_const_spec(a.shape) for a in consts],
        out_specs=[row(sw), row(sw), row(sw), row(lw)],
        out_shape=[jax.ShapeDtypeStruct((bsz, t_len, sw), BF16)] * 3
                  + [jax.ShapeDtypeStruct((bsz, t_len, lw), BF16)],
        scratch_shapes=[pltpu.VMEM((SUBLANES, lw), F32), pltpu.VMEM((1, lw), F32)],
        compiler_params=pltpu.CompilerParams(
            dimension_semantics=("arbitrary", "arbitrary"),
            vmem_limit_bytes=VMEM_LIMIT_BYTES),
        name="inproj_rglru",
    )(h, g1, w, gq, gk, grp, cw, cb, wg, ba, bi, lam)


def _attn_kernel(q_ref, k_ref, v_ref, uu_ref, o_ref, r_ref, acc_ref, rmax_ref):
    i = pl.program_id(1)
    bq = q_ref.shape[0]
    bk = bq
    n_pairs = q_ref.shape[1] // LANES
    word_lane = lax.broadcasted_iota(jnp.int32, (bk // 2, LANES), 1)
    first_head_bits = jnp.where(word_lane < HEAD_DIM, -1, 0)
    second_head_bits = jnp.where(word_lane < HEAD_DIM, 0, -1)
    row = lax.broadcasted_iota(jnp.int32, (bq, 2 * bk), 0)
    col = lax.broadcasted_iota(jnp.int32, (bq, 2 * bk), 1)
    causal = jnp.where(col >= bk, col - bk, col) < row
    sign_bit = jnp.int32(-2 ** 31)

    def stack_heads(ref, start, cols):
        words = pltpu.bitcast(ref[pl.ds(start, bk), cols], jnp.int32)
        return pltpu.bitcast(
            jnp.concatenate([words & first_head_bits, words & second_head_bits], axis=0), BF16)

    def sweep(blocks, fresh):
        starts = [pl.multiple_of(j * bk, bk) for j, _ in blocks]
        pairs = [slice(p * LANES, (p + 1) * LANES) for p in range(n_pairs)]
        ys, lhs, ss = {}, {}, {}
        r = [None if fresh else r_ref[p] for p in range(n_pairs)]
        acc = [None] * n_pairs

        def scores(n):
            rows = []
            for p, cols in enumerate(pairs):
                y = lax.dot_general(q_ref[:, cols], stack_heads(k_ref, starts[n], cols),
                                    (((1,), (1,)), ((), ())), preferred_element_type=F32)
                neg_abs = pltpu.bitcast(pltpu.bitcast(y, jnp.int32) | sign_bit, F32)
                sp = jnp.maximum(y, 0.0) + jnp.log2(1.0 + jnp.exp2(neg_abs))
                if blocks[n][1]:
                    sp = jnp.where(causal, sp, 0.0)
                hi, lo = _split_hi_lo(sp)
                rows += [jnp.concatenate([hi[:, :bk], lo[:, :bk]], axis=1),
                         jnp.concatenate([hi[:, bk:], lo[:, bk:]], axis=1)]
                ys[n, p] = y
            lhs[n] = jnp.concatenate(rows, axis=0)

        def suffix_sums(n):
            ss[n] = jnp.dot(lhs.pop(n), uu_ref[...], preferred_element_type=F32)

        def weights(n):
            for p, cols in enumerate(pairs):
                sa = ss[n][(2 * p) * bq:(2 * p + 1) * bq]
                sb = ss[n][(2 * p + 1) * bq:(2 * p + 2) * bq]
                suffix = jnp.concatenate([sa[:, :bk], sb[:, :bk]], axis=1)
                total = jnp.concatenate([sa[:, bk:], sb[:, bk:]], axis=1)
                log2_w = ys.pop((n, p)) + suffix
                if r[p] is not None:
                    log2_w = log2_w + r[p]
                w = jnp.exp2(log2_w)
                if blocks[n][1]:
                    w = jnp.where(causal, w, 0.0)
                r[p] = total if r[p] is None else r[p] + total
                part = jnp.dot(w.astype(BF16), stack_heads(v_ref, starts[n], cols),
                               preferred_element_type=F32)
                acc[p] = part if acc[p] is None else acc[p] + part
            del ss[n]

        for step in range(len(blocks) + 2):
            if step < len(blocks):
                scores(step)
            if 0 <= step - 1 < len(blocks):
                suffix_sums(step - 1)
            if 0 <= step - 2 < len(blocks):
                weights(step - 2)

        r_max = None
        for p, cols in enumerate(pairs):
            r_ref[p] = r[p]
            acc_ref[:, cols] = acc[p] if fresh else acc_ref[:, cols] + acc[p]
            r_max = r[p] if r_max is None else jnp.maximum(r_max, r[p])
        return jnp.max(r_max)

    n_first = 3

    @pl.when(i >= n_first - 1)
    def _():
        rmax_ref[0] = sweep([(i - n, n == 0) for n in range(n_first)], fresh=True)

    @pl.when(i < n_first - 1)
    def _():
        rmax_ref[0] = sweep([(i, True)], fresh=True)

    lax.while_loop(lambda c: jnp.logical_and(c[0] >= 0, c[1] > ZERO_WEIGHT_LOG2),
                   lambda c: (c[0] - 1, sweep([(c[0], False)], fresh=False)),
                   (jnp.where(i >= n_first - 1, i - n_first, i - 1), rmax_ref[0]))
    o_ref[...] = acc_ref[...].astype(BF16)


def _attention(q, k, v, bq):
    bsz, t_len, sw = q.shape
    n_pairs = sw // LANES
    j = lax.broadcasted_iota(jnp.int32, (2 * bq, 2 * bq), 0) % bq
    s = lax.broadcasted_iota(jnp.int32, (2 * bq, 2 * bq), 1)
    uu = jnp.where((s >= bq) | (j >= s), -1.0, 0.0).astype(BF16)
    blk = pl.BlockSpec((None, bq, sw), lambda b, i: (b, i, 0))
    seq = pl.BlockSpec((None, t_len, sw), lambda b, i: (b, 0, 0))
    return pl.pallas_call(
        _attn_kernel,
        grid=(bsz, t_len // bq),
        in_specs=[blk, seq, seq, _const_spec(uu.shape)],
        out_specs=blk,
        out_shape=jax.ShapeDtypeStruct((bsz, t_len, sw), BF16),
        scratch_shapes=[pltpu.VMEM((n_pairs, bq, 2 * bq), F32), pltpu.VMEM((bq, sw), F32),
                        pltpu.SMEM((1,), F32)],
        compiler_params=pltpu.CompilerParams(
            dimension_semantics=("arbitrary", "arbitrary"),
            vmem_limit_bytes=VMEM_LIMIT_BYTES),
        name="stickbreak",
    )(q, k, v, uu)


def _ffn_kernel(h_ref, osb_ref, olru_ref, wo_ref, g2_ref, wi_ref, cw_ref, cb_ref, wd_ref,
                out_ref, tail_ref, *, chunk, lookahead, down_group):
    @pl.when(pl.program_id(1) == 0)
    def _():
        tail_ref[...] = jnp.zeros_like(tail_ref)

    bm = h_ref.shape[0]
    sw = osb_ref.shape[1]
    dff = wd_ref.shape[0]
    taps = cw_ref.shape[0]
    h1 = h_ref[...] + (jnp.dot(osb_ref[...], wo_ref[:sw, :], preferred_element_type=F32)
                       + jnp.dot(olru_ref[...], wo_ref[sw:, :], preferred_element_type=F32))
    ms = jnp.mean(h1 * h1, axis=-1, keepdims=True)
    hn = (h1 * lax.rsqrt(ms + EPS) * g2_ref[...]).astype(BF16)

    def up_proj(j):
        return tuple(jnp.dot(hn, wi_ref[:, off + j * chunk:off + (j + 1) * chunk],
                             preferred_element_type=F32) for off in (0, dff))

    def conv(pre, cols):
        tail = tail_ref[:, cols]
        tail_ref[:, cols] = pre[bm - SUBLANES:, :]
        out = cw_ref[taps - 1:taps, cols] * pre + cb_ref[:, cols]
        for d in range(1, taps):
            out = out + cw_ref[taps - 1 - d:taps - d, cols] * _shift_rows_tail(pre, d, tail)
        return out

    n_chunks = dff // chunk
    acc = None
    acts = []
    pre = [up_proj(j) for j in range(min(lookahead, n_chunks))]
    for j in range(n_chunks):
        if j + lookahead < n_chunks:
            pre.append(up_proj(j + lookahead))
        u = conv(pre[j][0], slice(j * chunk, (j + 1) * chunk))
        g = conv(pre[j][1], slice(dff + j * chunk, dff + (j + 1) * chunk))
        acts.append((jax.nn.silu(g) * u).astype(BF16))
        if len(acts) == down_group or j == n_chunks - 1:
            first = j + 1 - len(acts)
            part = jnp.dot(jnp.concatenate(acts, axis=1), wd_ref[first * chunk:(j + 1) * chunk, :],
                           preferred_element_type=F32)
            acc = part if acc is None else acc + part
            acts = []
    out_ref[...] = h1 + acc


def _ffn(h, osb, olru, wo, g2, wi, cw, cb, wd, bm, chunk):
    bsz, t_len, d = h.shape
    sw = osb.shape[-1]
    lw = olru.shape[-1]
    row = lambda width: pl.BlockSpec((None, bm, width), lambda b, i: (b, i, 0))
    return pl.pallas_call(
        functools.partial(_ffn_kernel, chunk=chunk, lookahead=2, down_group=4),
        grid=(bsz, t_len // bm),
        in_specs=[row(d), row(sw), row(lw)] + [_const_spec(a.shape) for a in (wo, g2, wi, cw, cb, wd)],
        out_specs=row(d),
        out_shape=jax.ShapeDtypeStruct((bsz, t_len, d), F32),
        scratch_shapes=[pltpu.VMEM((SUBLANES, wi.shape[1]), F32)],
        compiler_params=pltpu.CompilerParams(
            dimension_semantics=("arbitrary", "arbitrary"),
            vmem_limit_bytes=VMEM_LIMIT_BYTES),
        name="outproj_convffn",
    )(h, osb, olru, wo, g2, wi, cw, cb, wd)


def _block_diag(w):
    n, c, d = w.shape
    eye = jnp.eye(n, dtype=w.dtype)
    return (eye[:, None, :, None] * w[:, :, None, :]).reshape(n * c, n * d)


def kernel(x, meta_tokens, norm1_g, w_in, q_norm_g, k_norm_g, conv_w, conv_b, w_rg_a, b_rg_a,
           w_rg_i, b_rg_i, lru_lambda, w_out, norm2_g, w_ffn_in, ffn_conv_w, ffn_conv_b, w_ffn_out):
    bsz, seq, d = x.shape
    total = seq + N_META
    t_len = -(-total // Q_BLOCK) * Q_BLOCK
    meta = jnp.broadcast_to(meta_tokens[None].astype(x.dtype), (bsz, N_META, d))
    pad = jnp.zeros((bsz, t_len - total, d), x.dtype)
    h = jnp.concatenate([meta, x, pad], axis=1)

    lw = conv_w.shape[-1]
    sw = (w_in.shape[-1] - 2 * lw) // 3
    n_heads = sw // HEAD_DIM
    bm = _pick_block(t_len, 640)
    chunk = 2 * LANES
    head_id = jnp.arange(sw) // HEAD_DIM
    grp = jnp.where(head_id[:, None] == head_id[None, :], 1.0 / HEAD_DIM, 0.0).astype(BF16)
    row2d = lambda a: a.reshape(1, -1)

    for layer in range(norm1_g.shape[0]):
        w_gates = jnp.concatenate(
            [_block_diag(w_rg_a[layer]), _block_diag(w_rg_i[layer])], axis=1).astype(BF16)
        q, k, v, o_lru = _inproj(
            h, row2d(norm1_g[layer]), w_in[layer].astype(BF16),
            row2d(jnp.tile(q_norm_g[layer], n_heads)), row2d(jnp.tile(k_norm_g[layer], n_heads)),
            grp, conv_w[layer], row2d(conv_b[layer]), w_gates,
            row2d(b_rg_a[layer]), row2d(b_rg_i[layer]), row2d(lru_lambda[layer]), bm)
        o_sb = _attention(q, k, v, Q_BLOCK)
        h = _ffn(h, o_sb, o_lru, w_out[layer].astype(BF16), row2d(norm2_g[layer]),
                 w_ffn_in[layer].astype(BF16), ffn_conv_w[layer], row2d(ffn_conv_b[layer]),
                 w_ffn_out[layer].astype(BF16), bm, chunk)

    return h[:, N_META:N_META + seq]
```

```python
import functools
import math

import jax
import jax.numpy as jnp
from jax import lax
from jax.experimental import pallas as pl
from jax.experimental.pallas import tpu as pltpu

N_META = 16
Q_BLOCK = 128
HEAD_DIM = 64
LRU_C = 8.0
EPS = 1e-6

LANES = 128
SUBLANES = 8
VMEM_LIMIT_BYTES = 56 * 1024 * 1024
LOG2E = math.log2(math.e)
ZERO_WEIGHT_LOG2 = -160.0

F32 = jnp.float32
BF16 = jnp.bfloat16


def _pick_block(n, target):
    best = Q_BLOCK
    for m in range(Q_BLOCK, min(n, target) + 1, Q_BLOCK):
        if n % m == 0:
            best = m
    return best


def _split_hi_lo(x):
    hi = x.astype(BF16)
    lo = (x - hi.astype(F32)).astype(BF16)
    return hi, lo


def _const_spec(shape):
    return pl.BlockSpec(shape, lambda *_: (0,) * len(shape), pipeline_mode=pl.Buffered(1))


def _shift_rows_tail(x, d, tail):
    r = pltpu.roll(x, d, 0)
    rt = pltpu.roll(tail, d, 0)
    rows = lax.broadcasted_iota(jnp.int32, tail.shape, 0)
    first = jnp.where(rows < d, rt, r[:SUBLANES])
    return jnp.concatenate([first, r[SUBLANES:]], axis=0)


def _linear_scan(a, b, carry):
    n = a.shape[0]
    sub = lax.broadcasted_iota(jnp.int32, a.shape, 0) % SUBLANES
    d = 1
    while d < SUBLANES:
        inside = sub >= d
        b = a * jnp.where(inside, pltpu.roll(b, d, 0), 0.0) + b
        a = a * jnp.where(inside, pltpu.roll(a, d, 0), 1.0)
        d *= 2
    groups = []
    for g in range(n // SUBLANES):
        rows = slice(g * SUBLANES, (g + 1) * SUBLANES)
        hg = a[rows] * carry + b[rows]
        carry = hg[SUBLANES - 1:]
        groups.append(hg)
    return jnp.concatenate(groups, axis=0)


def _sequence_rows(x_ref, meta_ref, edge_ref, n_pad):
    b, i = pl.program_id(0), pl.program_id(1)
    first_block, last_block = i == 0, i == pl.num_programs(1) - 1
    first_window = jnp.logical_and(b == 0, first_block)
    last_window = jnp.logical_and(b == pl.num_programs(0) - 1, last_block)
    bm = x_ref.shape[0]
    win = x_ref[...]

    @pl.when(first_window)
    def _():
        edge_ref[N_META:, :] = win[:bm - N_META]
        edge_ref[:N_META, :] = meta_ref[...]

    @pl.when(last_window)
    def _():
        edge_ref[:bm - n_pad, :] = win[n_pad:]
        edge_ref[bm - n_pad:, :] = jnp.zeros((n_pad, win.shape[1]), win.dtype)

    h = jnp.where(jnp.logical_or(first_window, last_window), edge_ref[...], win)
    top = jnp.where(first_block, meta_ref[...], h[:N_META])
    bottom = jnp.where(last_block, 0.0, h[bm - n_pad:])
    return jnp.concatenate([top, h[N_META:bm - n_pad], bottom], axis=0)


def _inproj_kernel(x_ref, meta_ref, g1_ref, w_ref, gq_ref, gk_ref, grp_ref,
                   cw_ref, cb_ref, wg_ref, ba_ref, bi_ref, lam_ref,
                   q_ref, k_ref, v_ref, o_ref, tail_ref, carry_ref, edge_ref, *, n_pad):
    @pl.when(pl.program_id(1) == 0)
    def _():
        tail_ref[...] = jnp.zeros_like(tail_ref)
        carry_ref[...] = jnp.zeros_like(carry_ref)

    sw = q_ref.shape[-1]
    lw = o_ref.shape[-1]
    bm = x_ref.shape[0]
    h = _sequence_rows(x_ref, meta_ref, edge_ref, n_pad)
    ms = jnp.mean(h * h, axis=-1, keepdims=True)
    hn = (h * lax.rsqrt(ms + EPS) * g1_ref[...]).astype(BF16)

    def proj(lo, hi):
        return jnp.dot(hn, w_ref[:, lo:hi], preferred_element_type=F32)

    def head_norm(t, g):
        hms = jnp.dot((t * t).astype(BF16), grp_ref[...], preferred_element_type=F32)
        return t * lax.rsqrt(hms + EPS) * g

    xr = proj(3 * sw, 3 * sw + lw)
    yg = proj(3 * sw + lw, 3 * sw + 2 * lw)
    tail = tail_ref[...]
    taps = cw_ref.shape[0]
    xc = cw_ref[taps - 1:taps, :] * xr + cb_ref[...]
    for d in range(1, taps):
        xc = xc + cw_ref[taps - 1 - d:taps - d, :] * _shift_rows_tail(xr, d, tail)
    tail_ref[...] = xr[bm - SUBLANES:]
    gates = jnp.dot(xc.astype(BF16), wg_ref[...], preferred_element_type=F32)

    q_raw = proj(0, sw)
    k_raw = proj(sw, 2 * sw)
    v_ref[...] = proj(2 * sw, 3 * sw).astype(BF16)
    q_ref[...] = head_norm(q_raw, gq_ref[...] * (HEAD_DIM ** -0.5 * LOG2E)).astype(BF16)
    k_ref[...] = head_norm(k_raw, gk_ref[...]).astype(BF16)

    r_gate = jax.nn.sigmoid(gates[:, :lw] + ba_ref[...])
    i_gate = jax.nn.sigmoid(gates[:, lw:] + bi_ref[...])
    lam = lam_ref[...]
    neg_softplus = jnp.maximum(-lam, 0.0) + jnp.log(1.0 + jnp.exp(-jnp.abs(lam)))
    a = jnp.exp((-LRU_C) * r_gate * neg_softplus)
    b = jnp.sqrt((1.0 - a) * (1.0 + a)) * (i_gate * xc)
    hs = _linear_scan(a, b, carry_ref[...])
    carry_ref[...] = hs[bm - 1:]
    o_ref[...] = (jax.nn.gelu(yg, approximate=True) * hs).astype(BF16)


def _inproj(x, meta, g1, w, gq, gk, grp, cw, cb, wg, ba, bi, lam, t_len, bm):
    bsz, seq, d = x.shape
    sw = grp.shape[0]
    lw = cw.shape[1]
    n_blocks = t_len // bm
    n_pad = t_len - seq - N_META
    assert bsz * n_blocks > 1 and n_pad % SUBLANES == 0 and n_pad + N_META <= bm, (t_len, bm)
    rows = bsz * seq
    window = pl.BlockSpec(
        (pl.Element(bm), pl.Element(d)),
        lambda b, i: (pl.multiple_of(jnp.clip(b * seq + i * bm - N_META, 0, rows - bm), N_META), 0))
    row = lambda width: pl.BlockSpec((None, bm, width), lambda b, i: (b, i, 0))
    consts = (meta, g1, w, gq, gk, grp, cw, cb, wg, ba, bi, lam)
    return pl.pallas_call(
        functools.partial(_inproj_kernel, n_pad=n_pad),
        grid=(bsz, n_blocks),
        in_specs=[window] + [_const_spec(a.shape) for a in consts],
        out_specs=[row(sw), row(sw), row(sw), row(lw)],
        out_shape=[jax.ShapeDtypeStruct((bsz, t_len, sw), BF16)] * 3
                  + [jax.ShapeDtypeStruct((bsz, t_len, lw), BF16)],
        scratch_shapes=[pltpu.VMEM((SUBLANES, lw), F32), pltpu.VMEM((1, lw), F32),
                        pltpu.VMEM((bm, d), F32)],
        compiler_params=pltpu.CompilerParams(
            dimension_semantics=("arbitrary", "arbitrary"),
            vmem_limit_bytes=VMEM_LIMIT_BYTES),
        name="inproj_rglru",
    )(x.reshape(rows, d), *consts)


def _attn_kernel(q_ref, k_ref, v_ref, uu_ref, o_ref, r_ref, acc_ref, rmax_ref):
    i = pl.program_id(1)
    bq = q_ref.shape[0]
    bk = bq
    n_pairs = q_ref.shape[1] // LANES
    word_lane = lax.broadcasted_iota(jnp.int32, (bk // 2, LANES), 1)
    first_head_bits = jnp.where(word_lane < HEAD_DIM, -1, 0)
    second_head_bits = jnp.where(word_lane < HEAD_DIM, 0, -1)
    row = lax.broadcasted_iota(jnp.int32, (bq, 2 * bk), 0)
    col = lax.broadcasted_iota(jnp.int32, (bq, 2 * bk), 1)
    causal = jnp.where(col >= bk, col - bk, col) < row
    sign_bit = jnp.int32(-2 ** 31)

    def stack_heads(ref, start, cols):
        words = pltpu.bitcast(ref[pl.ds(start, bk), cols], jnp.int32)
        return pltpu.bitcast(
            jnp.concatenate([words & first_head_bits, words & second_head_bits], axis=0), BF16)

    def sweep(blocks, fresh):
        starts = [pl.multiple_of(j * bk, bk) for j, _ in blocks]
        pairs = [slice(p * LANES, (p + 1) * LANES) for p in range(n_pairs)]
        ys, lhs, ss = {}, {}, {}
        r = [None if fresh else r_ref[p] for p in range(n_pairs)]
        acc = [None] * n_pairs

        def scores(n):
            rows = []
            for p, cols in enumerate(pairs):
                y = lax.dot_general(q_ref[:, cols], stack_heads(k_ref, starts[n], cols),
                                    (((1,), (1,)), ((), ())), preferred_element_type=F32)
                neg_abs = pltpu.bitcast(pltpu.bitcast(y, jnp.int32) | sign_bit, F32)
                sp = jnp.maximum(y, 0.0) + jnp.log2(1.0 + jnp.exp2(neg_abs))
                if blocks[n][1]:
                    sp = jnp.where(causal, sp, 0.0)
                hi, lo = _split_hi_lo(sp)
                rows += [jnp.concatenate([hi[:, :bk], lo[:, :bk]], axis=1),
                         jnp.concatenate([hi[:, bk:], lo[:, bk:]], axis=1)]
                ys[n, p] = y
            lhs[n] = jnp.concatenate(rows, axis=0)

        def suffix_sums(n):
            ss[n] = jnp.dot(lhs.pop(n), uu_ref[...], preferred_element_type=F32)

        def weights(n):
            for p, cols in enumerate(pairs):
                sa = ss[n][(2 * p) * bq:(2 * p + 1) * bq]
                sb = ss[n][(2 * p + 1) * bq:(2 * p + 2) * bq]
                suffix = jnp.concatenate([sa[:, :bk], sb[:, :bk]], axis=1)
                total = jnp.concatenate([sa[:, bk:], sb[:, bk:]], axis=1)
                log2_w = ys.pop((n, p)) + suffix
                if r[p] is not None:
                    log2_w = log2_w + r[p]
                w = jnp.exp2(log2_w)
                if blocks[n][1]:
                    w = jnp.where(causal, w, 0.0)
                r[p] = total if r[p] is None else r[p] + total
                part = jnp.dot(w.astype(BF16), stack_heads(v_ref, starts[n], cols),
                               preferred_element_type=F32)
                acc[p] = part if acc[p] is None else acc[p] + part
            del ss[n]

        for step in range(len(blocks) + 2):
            if step < len(blocks):
                scores(step)
            if 0 <= step - 1 < len(blocks):
                suffix_sums(step - 1)
            if 0 <= step - 2 < len(blocks):
                weights(step - 2)

        r_max = None
        for p, cols in enumerate(pairs):
            r_ref[p] = r[p]
            acc_ref[:, cols] = acc[p] if fresh else acc_ref[:, cols] + acc[p]
            r_max = r[p] if r_max is None else jnp.maximum(r_max, r[p])
        return jnp.max(r_max)

    n_first = 3

    @pl.when(i >= n_first - 1)
    def _():
        rmax_ref[0] = sweep([(i - n, n == 0) for n in range(n_first)], fresh=True)

    @pl.when(i < n_first - 1)
    def _():
        rmax_ref[0] = sweep([(i, True)], fresh=True)

    lax.while_loop(lambda c: jnp.logical_and(c[0] >= 0, c[1] > ZERO_WEIGHT_LOG2),
                   lambda c: (c[0] - 1, sweep([(c[0], False)], fresh=False)),
                   (jnp.where(i >= n_first - 1, i - n_first, i - 1), rmax_ref[0]))
    o_ref[...] = acc_ref[...].astype(BF16)


def _attention(q, k, v, bq):
    bsz, t_len, sw = q.shape
    n_pairs = sw // LANES
    j = lax.broadcasted_iota(jnp.int32, (2 * bq, 2 * bq), 0) % bq
    s = lax.broadcasted_iota(jnp.int32, (2 * bq, 2 * bq), 1)
    uu = jnp.where((s >= bq) | (j >= s), -1.0, 0.0).astype(BF16)
    blk = pl.BlockSpec((None, bq, sw), lambda b, i: (b, i, 0))
    seq = pl.BlockSpec((None, t_len, sw), lambda b, i: (b, 0, 0))
    return pl.pallas_call(
        _attn_kernel,
        grid=(bsz, t_len // bq),
        in_specs=[blk, seq, seq, _const_spec(uu.shape)],
        out_specs=blk,
        out_shape=jax.ShapeDtypeStruct((bsz, t_len, sw), BF16),
        scratch_shapes=[pltpu.VMEM((n_pairs, bq, 2 * bq), F32), pltpu.VMEM((bq, sw), F32),
                        pltpu.SMEM((1,), F32)],
        compiler_params=pltpu.CompilerParams(
            dimension_semantics=("arbitrary", "arbitrary"),
            vmem_limit_bytes=VMEM_LIMIT_BYTES),
        name="stickbreak",
    )(q, k, v, uu)


def _ffn_kernel(x_ref, osb_ref, olru_ref, meta_ref, osb_meta_ref, olru_meta_ref,
                wo_ref, g2_ref, wi_ref, cw_ref, cb_ref, wd_ref,
                out_ref, tail_ref, *, chunk, lookahead, down_group):
    bm = x_ref.shape[0]
    sw = osb_ref.shape[-1]
    dff = wd_ref.shape[0]
    taps = cw_ref.shape[0]

    def mix_and_norm(resid, osb, olru):
        h1 = resid + (jnp.dot(osb, wo_ref[:sw, :], preferred_element_type=F32)
                      + jnp.dot(olru, wo_ref[sw:, :], preferred_element_type=F32))
        ms = jnp.mean(h1 * h1, axis=-1, keepdims=True)
        return h1, (h1 * lax.rsqrt(ms + EPS) * g2_ref[...]).astype(BF16)

    @pl.when(pl.program_id(1) == 0)
    def _():
        _, hn_meta = mix_and_norm(meta_ref[...], osb_meta_ref[0], olru_meta_ref[0])
        pre_meta = jnp.dot(hn_meta, wi_ref[...], preferred_element_type=F32)
        tail_ref[...] = pre_meta[N_META - SUBLANES:, :]

    h1, hn = mix_and_norm(x_ref[...], osb_ref[0], olru_ref[0])

    def up_proj(j):
        return tuple(jnp.dot(hn, wi_ref[:, off + j * chunk:off + (j + 1) * chunk],
                             preferred_element_type=F32) for off in (0, dff))

    def conv(pre, cols):
        tail = tail_ref[:, cols]
        tail_ref[:, cols] = pre[bm - SUBLANES:, :]
        out = cw_ref[taps - 1:taps, cols] * pre + cb_ref[:, cols]
        for d in range(1, taps):
            out = out + cw_ref[taps - 1 - d:taps - d, cols] * _shift_rows_tail(pre, d, tail)
        return out

    n_chunks = dff // chunk
    acc = None
    acts = []
    pre = [up_proj(j) for j in range(min(lookahead, n_chunks))]
    for j in range(n_chunks):
        if j + lookahead < n_chunks:
            pre.append(up_proj(j + lookahead))
        u = conv(pre[j][0], slice(j * chunk, (j + 1) * chunk))
        g = conv(pre[j][1], slice(dff + j * chunk, dff + (j + 1) * chunk))
        acts.append((jax.nn.silu(g) * u).astype(BF16))
        if len(acts) == down_group or j == n_chunks - 1:
            first = j + 1 - len(acts)
            part = jnp.dot(jnp.concatenate(acts, axis=1), wd_ref[first * chunk:(j + 1) * chunk, :],
                           preferred_element_type=F32)
            acc = part if acc is None else acc + part
            acts = []
    out_ref[...] = h1 + acc


def _ffn(x, osb, olru, meta, wo, g2, wi, cw, cb, wd, bm, chunk):
    bsz, seq, d = x.shape
    row = pl.BlockSpec((None, bm, d), lambda b, i: (b, i, 0))

    def window(rows, width, first_row):
        return pl.BlockSpec((pl.Element(1), pl.Element(rows), pl.Element(width)),
                            lambda b, i: (b, pl.multiple_of(first_row(i), N_META), 0))

    mixer_specs = [window(bm, a.shape[-1], lambda i: N_META + i * bm) for a in (osb, olru)]
    meta_specs = [window(N_META, a.shape[-1], lambda i: 0) for a in (osb, olru)]
    return pl.pallas_call(
        functools.partial(_ffn_kernel, chunk=chunk, lookahead=2, down_group=4),
        grid=(bsz, seq // bm),
        in_specs=[row] + mixer_specs + [_const_spec(meta.shape)] + meta_specs
                 + [_const_spec(a.shape) for a in (wo, g2, wi, cw, cb, wd)],
        out_specs=row,
        out_shape=jax.ShapeDtypeStruct((bsz, seq, d), F32),
        scratch_shapes=[pltpu.VMEM((SUBLANES, wi.shape[1]), F32)],
        compiler_params=pltpu.CompilerParams(
            dimension_semantics=("arbitrary", "arbitrary"),
            vmem_limit_bytes=VMEM_LIMIT_BYTES),
        name="outproj_convffn",
    )(x, osb, olru, meta, osb, olru, wo, g2, wi, cw, cb, wd)


def _block_diag(w):
    n, c, d = w.shape
    eye = jnp.eye(n, dtype=w.dtype)
    return (eye[:, None, :, None] * w[:, :, None, :]).reshape(n * c, n * d)


def kernel(x, meta_tokens, norm1_g, w_in, q_norm_g, k_norm_g, conv_w, conv_b, w_rg_a, b_rg_a,
           w_rg_i, b_rg_i, lru_lambda, w_out, norm2_g, w_ffn_in, ffn_conv_w, ffn_conv_b, w_ffn_out):
    bsz, seq, d = x.shape
    t_len = -(-(seq + N_META) // Q_BLOCK) * Q_BLOCK
    meta = meta_tokens.astype(x.dtype)

    assert norm1_g.shape[0] == 1 and seq % Q_BLOCK == 0, (norm1_g.shape, seq)
    layer = 0
    lw = conv_w.shape[-1]
    sw = (w_in.shape[-1] - 2 * lw) // 3
    n_heads = sw // HEAD_DIM
    chunk = 2 * LANES
    head_id = jnp.arange(sw) // HEAD_DIM
    grp = jnp.where(head_id[:, None] == head_id[None, :], 1.0 / HEAD_DIM, 0.0).astype(BF16)
    row2d = lambda a: a.reshape(1, -1)

    w_gates = jnp.concatenate(
        [_block_diag(w_rg_a[layer]), _block_diag(w_rg_i[layer])], axis=1).astype(BF16)
    q, k, v, o_lru = _inproj(
        x, meta, row2d(norm1_g[layer]), w_in[layer].astype(BF16),
        row2d(jnp.tile(q_norm_g[layer], n_heads)), row2d(jnp.tile(k_norm_g[layer], n_heads)),
        grp, conv_w[layer], row2d(conv_b[layer]), w_gates,
        row2d(b_rg_a[layer]), row2d(b_rg_i[layer]), row2d(lru_lambda[layer]),
        t_len, _pick_block(t_len, 640))
    o_sb = _attention(q, k, v, Q_BLOCK)
    return _ffn(x, o_sb, o_lru, meta, w_out[layer].astype(BF16),
                row2d(norm2_g[layer]), w_ffn_in[layer].astype(BF16), ffn_conv_w[layer],
                row2d(ffn_conv_b[layer]), w_ffn_out[layer].astype(BF16),
                _pick_block(seq, 1024), chunk)
```

```python
import functools
import math

import jax
import jax.numpy as jnp
from jax import lax
from jax.experimental import pallas as pl
from jax.experimental.pallas import tpu as pltpu

N_META = 16
Q_BLOCK = 128
HEAD_DIM = 64
LRU_C = 8.0
EPS = 1e-6

LANES = 128
SUBLANES = 8
VMEM_LIMIT_BYTES = 56 * 1024 * 1024
LOG2E = math.log2(math.e)
ZERO_WEIGHT_LOG2 = -160.0

F32 = jnp.float32
BF16 = jnp.bfloat16


def _pick_block(n, target):
    best = Q_BLOCK
    for m in range(Q_BLOCK, min(n, target) + 1, Q_BLOCK):
        if n % m == 0:
            best = m
    return best


def _split_hi_lo(x):
    hi = x.astype(BF16)
    lo = (x - hi.astype(F32)).astype(BF16)
    return hi, lo


def _const_spec(shape):
    return pl.BlockSpec(shape, lambda *_: (0,) * len(shape), pipeline_mode=pl.Buffered(1))


def _shift_rows_tail(x, d, tail):
    r = pltpu.roll(x, d, 0)
    rt = pltpu.roll(tail, d, 0)
    rows = lax.broadcasted_iota(jnp.int32, tail.shape, 0)
    first = jnp.where(rows < d, rt, r[:SUBLANES])
    return jnp.concatenate([first, r[SUBLANES:]], axis=0)


def _linear_scan(a, b, carry):
    n = a.shape[0]
    sub = lax.broadcasted_iota(jnp.int32, a.shape, 0) % SUBLANES
    d = 1
    while d < SUBLANES:
        inside = sub >= d
        b = a * jnp.where(inside, pltpu.roll(b, d, 0), 0.0) + b
        a = a * jnp.where(inside, pltpu.roll(a, d, 0), 1.0)
        d *= 2
    groups = []
    for g in range(n // SUBLANES):
        rows = slice(g * SUBLANES, (g + 1) * SUBLANES)
        hg = a[rows] * carry + b[rows]
        carry = hg[SUBLANES - 1:]
        groups.append(hg)
    return jnp.concatenate(groups, axis=0)


def _sequence_rows(x_ref, meta_ref, edge_ref, n_pad):
    b, i = pl.program_id(0), pl.program_id(1)
    first_block, last_block = i == 0, i == pl.num_programs(1) - 1
    first_window = jnp.logical_and(b == 0, first_block)
    last_window = jnp.logical_and(b == pl.num_programs(0) - 1, last_block)
    bm = x_ref.shape[0]
    win = x_ref[...]

    @pl.when(first_window)
    def _():
        edge_ref[N_META:, :] = win[:bm - N_META]
        edge_ref[:N_META, :] = meta_ref[...]

    @pl.when(last_window)
    def _():
        edge_ref[:bm - n_pad, :] = win[n_pad:]
        edge_ref[bm - n_pad:, :] = jnp.zeros((n_pad, win.shape[1]), win.dtype)

    h = jnp.where(jnp.logical_or(first_window, last_window), edge_ref[...], win)
    top = jnp.where(first_block, meta_ref[...], h[:N_META])
    bottom = jnp.where(last_block, 0.0, h[bm - n_pad:])
    return jnp.concatenate([top, h[N_META:bm - n_pad], bottom], axis=0)


def _inproj_kernel(x_ref, meta_ref, g1_ref, w_ref, gq_ref, gk_ref, grp_ref,
                   cw_ref, cb_ref, wg_ref, ba_ref, bi_ref, lam_ref,
                   q_ref, k_ref, v_ref, o_ref, tail_ref, carry_ref, edge_ref, *, n_pad):
    @pl.when(pl.program_id(1) == 0)
    def _():
        tail_ref[...] = jnp.zeros_like(tail_ref)
        carry_ref[...] = jnp.zeros_like(carry_ref)

    sw = q_ref.shape[-1]
    lw = o_ref.shape[-1]
    bm = x_ref.shape[0]
    h = _sequence_rows(x_ref, meta_ref, edge_ref, n_pad)
    ms = jnp.mean(h * h, axis=-1, keepdims=True)
    hn = (h * lax.rsqrt(ms + EPS) * g1_ref[...]).astype(BF16)

    def proj(lo, hi):
        return jnp.dot(hn, w_ref[:, lo:hi], preferred_element_type=F32)

    def head_norm(t, g):
        hms = jnp.dot((t * t).astype(BF16), grp_ref[...], preferred_element_type=F32)
        return t * lax.rsqrt(hms + EPS) * g

    xr = proj(3 * sw, 3 * sw + lw)
    yg = proj(3 * sw + lw, 3 * sw + 2 * lw)
    tail = tail_ref[...]
    taps = cw_ref.shape[0]
    xc = cw_ref[taps - 1:taps, :] * xr + cb_ref[...]
    for d in range(1, taps):
        xc = xc + cw_ref[taps - 1 - d:taps - d, :] * _shift_rows_tail(xr, d, tail)
    tail_ref[...] = xr[bm - SUBLANES:]
    gates = jnp.dot(xc.astype(BF16), wg_ref[...], preferred_element_type=F32)

    q_raw = proj(0, sw)
    k_raw = proj(sw, 2 * sw)
    v_ref[...] = proj(2 * sw, 3 * sw).astype(BF16)
    q_ref[...] = head_norm(q_raw, gq_ref[...] * (HEAD_DIM ** -0.5 * LOG2E)).astype(BF16)
    k_ref[...] = head_norm(k_raw, gk_ref[...]).astype(BF16)

    r_gate = jax.nn.sigmoid(gates[:, :lw] + ba_ref[...])
    i_gate = jax.nn.sigmoid(gates[:, lw:] + bi_ref[...])
    lam = lam_ref[...]
    neg_softplus = jnp.maximum(-lam, 0.0) + jnp.log(1.0 + jnp.exp(-jnp.abs(lam)))
    a = jnp.exp((-LRU_C) * r_gate * neg_softplus)
    b = jnp.sqrt((1.0 - a) * (1.0 + a)) * (i_gate * xc)
    hs = _linear_scan(a, b, carry_ref[...])
    carry_ref[...] = hs[bm - 1:]
    o_ref[...] = (jax.nn.gelu(yg, approximate=True) * hs).astype(BF16)


def _inproj(x, meta, g1, w, gq, gk, grp, cw, cb, wg, ba, bi, lam, t_len, bm):
    bsz, seq, d = x.shape
    sw = grp.shape[0]
    lw = cw.shape[1]
    n_blocks = t_len // bm
    n_pad = t_len - seq - N_META
    assert bsz * n_blocks > 1 and n_pad % SUBLANES == 0 and n_pad + N_META <= bm, (t_len, bm)
    rows = bsz * seq
    window = pl.BlockSpec(
        (pl.Element(bm), pl.Element(d)),
        lambda b, i: (pl.multiple_of(jnp.clip(b * seq + i * bm - N_META, 0, rows - bm), N_META), 0))
    row = lambda width: pl.BlockSpec((None, bm, width), lambda b, i: (b, i, 0))
    consts = (meta, g1, w, gq, gk, grp, cw, cb, wg, ba, bi, lam)
    return pl.pallas_call(
        functools.partial(_inproj_kernel, n_pad=n_pad),
        grid=(bsz, n_blocks),
        in_specs=[window] + [_const_spec(a.shape) for a in consts],
        out_specs=[row(sw), row(sw), row(sw), row(lw)],
        out_shape=[jax.ShapeDtypeStruct((bsz, t_len, sw), BF16)] * 3
                  + [jax.ShapeDtypeStruct((bsz, t_len, lw), BF16)],
        scratch_shapes=[pltpu.VMEM((SUBLANES, lw), F32), pltpu.VMEM((1, lw), F32),
                        pltpu.VMEM((bm, d), F32)],
        compiler_params=pltpu.CompilerParams(
            dimension_semantics=("arbitrary", "arbitrary"),
            vmem_limit_bytes=VMEM_LIMIT_BYTES),
        name="inproj_rglru",
    )(x.reshape(rows, d), *consts)


def _attn_kernel(q_ref, k_ref, v_ref, uu_ref, o_ref, r_ref, acc_ref, rmax_ref, *, bq):
    i = pl.program_id(1)
    bk = bq
    group = q_ref.shape[0] // bq
    n_pairs = q_ref.shape[1] // LANES
    pairs = [slice(p * LANES, (p + 1) * LANES) for p in range(n_pairs)]
    word_lane = lax.broadcasted_iota(jnp.int32, (bk // 2, LANES), 1)
    first_head_bits = jnp.where(word_lane < HEAD_DIM, -1, 0)
    second_head_bits = jnp.where(word_lane < HEAD_DIM, 0, -1)
    row = lax.broadcasted_iota(jnp.int32, (bq, 2 * bk), 0)
    col = lax.broadcasted_iota(jnp.int32, (bq, 2 * bk), 1)
    causal = jnp.where(col >= bk, col - bk, col) < row
    sign_bit = jnp.int32(-2 ** 31)

    def stack_heads(ref, start, cols):
        words = pltpu.bitcast(ref[pl.ds(start, bk), cols], jnp.int32)
        return pltpu.bitcast(
            jnp.concatenate([words & first_head_bits, words & second_head_bits], axis=0), BF16)

    def sweep(tasks, fresh):
        members = sorted({t[0] for t in tasks})
        ys, lhs, ss, stacked = {}, {}, {}, {}
        r = {(g, p): None if fresh else r_ref[g, p] for g in members for p in range(n_pairs)}
        acc = {(g, p): None for g in members for p in range(n_pairs)}

        def key_block(ref, tag, t, p):
            _, j, _, key_id = tasks[t]
            if (tag, key_id, p) not in stacked:
                stacked[tag, key_id, p] = stack_heads(ref, pl.multiple_of(j * bk, bk), pairs[p])
            return stacked[tag, key_id, p]

        def scores(t):
            g, _, on_diagonal, _ = tasks[t]
            rows = []
            for p, cols in enumerate(pairs):
                y = lax.dot_general(q_ref[g * bq:(g + 1) * bq, cols], key_block(k_ref, "k", t, p),
                                    (((1,), (1,)), ((), ())), preferred_element_type=F32)
                neg_abs = pltpu.bitcast(pltpu.bitcast(y, jnp.int32) | sign_bit, F32)
                sp = jnp.maximum(y, 0.0) + jnp.log2(1.0 + jnp.exp2(neg_abs))
                if on_diagonal:
                    sp = jnp.where(causal, sp, 0.0)
                hi, lo = _split_hi_lo(sp)
                rows += [jnp.concatenate([hi[:, :bk], lo[:, :bk]], axis=1),
                         jnp.concatenate([hi[:, bk:], lo[:, bk:]], axis=1)]
                ys[t, p] = y
            lhs[t] = jnp.concatenate(rows, axis=0)

        def suffix_sums(t):
            ss[t] = jnp.dot(lhs.pop(t), uu_ref[...], preferred_element_type=F32)

        def weights(t):
            g, _, on_diagonal, _ = tasks[t]
            for p in range(n_pairs):
                sa = ss[t][(2 * p) * bq:(2 * p + 1) * bq]
                sb = ss[t][(2 * p + 1) * bq:(2 * p + 2) * bq]
                suffix = jnp.concatenate([sa[:, :bk], sb[:, :bk]], axis=1)
                total = jnp.concatenate([sa[:, bk:], sb[:, bk:]], axis=1)
                log2_w = ys.pop((t, p)) + suffix
                if r[g, p] is not None:
                    log2_w = log2_w + r[g, p]
                w = jnp.exp2(log2_w)
                if on_diagonal:
                    w = jnp.where(causal, w, 0.0)
                r[g, p] = total if r[g, p] is None else r[g, p] + total
                part = jnp.dot(w.astype(BF16), key_block(v_ref, "v", t, p),
                               preferred_element_type=F32)
                acc[g, p] = part if acc[g, p] is None else acc[g, p] + part
            del ss[t]

        for step in range(len(tasks) + 2):
            if step < len(tasks):
                scores(step)
            if 0 <= step - 1 < len(tasks):
                suffix_sums(step - 1)
            if 0 <= step - 2 < len(tasks):
                weights(step - 2)

        r_max = {}
        for g in members:
            rows = slice(g * bq, (g + 1) * bq)
            top = None
            for p, cols in enumerate(pairs):
                r_ref[g, p] = r[g, p]
                acc_ref[rows, cols] = acc[g, p] if fresh else acc_ref[rows, cols] + acc[g, p]
                top = r[g, p] if top is None else jnp.maximum(top, r[g, p])
            r_max[g] = jnp.max(top)
        return r_max

    depth = 3

    def first_tiles(only_existing):
        return [(g, group * i + g - n, n == 0, g - n)
                for n in range(depth) for g in range(group) if g - n >= 0 or not only_existing]

    @pl.when(i > 0)
    def _():
        for g, top in sweep(first_tiles(False), fresh=True).items():
            rmax_ref[g] = top

    @pl.when(i == 0)
    def _():
        for g, top in sweep(first_tiles(True), fresh=True).items():
            rmax_ref[g] = top

    for g in range(group):
        lax.while_loop(lambda c: jnp.logical_and(c[0] >= 0, c[1] > ZERO_WEIGHT_LOG2),
                       lambda c, g=g: (c[0] - 1, sweep([(g, c[0], False, None)], fresh=False)[g]),
                       (group * i + g - depth, rmax_ref[g]))
    o_ref[...] = acc_ref[...].astype(BF16)


def _attention(q, k, v, bq):
    bsz, t_len, sw = q.shape
    n_pairs = sw // LANES
    n_blocks = t_len // bq
    group = max(g for g in range(1, 6) if n_blocks % g == 0)
    j = lax.broadcasted_iota(jnp.int32, (2 * bq, 2 * bq), 0) % bq
    s = lax.broadcasted_iota(jnp.int32, (2 * bq, 2 * bq), 1)
    uu = jnp.where((s >= bq) | (j >= s), -1.0, 0.0).astype(BF16)
    blk = pl.BlockSpec((None, group * bq, sw), lambda b, i: (b, i, 0))
    seq = pl.BlockSpec((None, t_len, sw), lambda b, i: (b, 0, 0))
    return pl.pallas_call(
        functools.partial(_attn_kernel, bq=bq),
        grid=(bsz, n_blocks // group),
        in_specs=[blk, seq, seq, _const_spec(uu.shape)],
        out_specs=blk,
        out_shape=jax.ShapeDtypeStruct((bsz, t_len, sw), BF16),
        scratch_shapes=[pltpu.VMEM((group, n_pairs, bq, 2 * bq), F32),
                        pltpu.VMEM((group * bq, sw), F32), pltpu.SMEM((group,), F32)],
        compiler_params=pltpu.CompilerParams(
            dimension_semantics=("arbitrary", "arbitrary"),
            vmem_limit_bytes=VMEM_LIMIT_BYTES),
        name="stickbreak",
    )(q, k, v, uu)


def _ffn_kernel(x_ref, osb_ref, olru_ref, meta_ref, osb_meta_ref, olru_meta_ref,
                wo_ref, g2_ref, wi_ref, cw_ref, cb_ref, wd_ref,
                out_ref, tail_ref, *, chunk, lookahead, down_group):
    bm = x_ref.shape[0]
    sw = osb_ref.shape[-1]
    dff = wd_ref.shape[0]
    taps = cw_ref.shape[0]

    def mix_and_norm(resid, osb, olru):
        h1 = resid + (jnp.dot(osb, wo_ref[:sw, :], preferred_element_type=F32)
                      + jnp.dot(olru, wo_ref[sw:, :], preferred_element_type=F32))
        ms = jnp.mean(h1 * h1, axis=-1, keepdims=True)
        return h1, (h1 * lax.rsqrt(ms + EPS) * g2_ref[...]).astype(BF16)

    @pl.when(pl.program_id(1) == 0)
    def _():
        _, hn_meta = mix_and_norm(meta_ref[...], osb_meta_ref[0], olru_meta_ref[0])
        pre_meta = jnp.dot(hn_meta, wi_ref[...], preferred_element_type=F32)
        tail_ref[...] = pre_meta[N_META - SUBLANES:, :]

    h1, hn = mix_and_norm(x_ref[...], osb_ref[0], olru_ref[0])

    def up_proj(j):
        return tuple(jnp.dot(hn, wi_ref[:, off + j * chunk:off + (j + 1) * chunk],
                             preferred_element_type=F32) for off in (0, dff))

    def conv(pre, cols):
        tail = tail_ref[:, cols]
        tail_ref[:, cols] = pre[bm - SUBLANES:, :]
        out = cw_ref[taps - 1:taps, cols] * pre + cb_ref[:, cols]
        for d in range(1, taps):
            out = out + cw_ref[taps - 1 - d:taps - d, cols] * _shift_rows_tail(pre, d, tail)
        return out

    n_chunks = dff // chunk
    acc = None
    acts = []
    pre = [up_proj(j) for j in range(min(lookahead, n_chunks))]
    for j in range(n_chunks):
        if j + lookahead < n_chunks:
            pre.append(up_proj(j + lookahead))
        u = conv(pre[j][0], slice(j * chunk, (j + 1) * chunk))
        g = conv(pre[j][1], slice(dff + j * chunk, dff + (j + 1) * chunk))
        acts.append((jax.nn.silu(g) * u).astype(BF16))
        if len(acts) == down_group or j == n_chunks - 1:
            first = j + 1 - len(acts)
            part = jnp.dot(jnp.concatenate(acts, axis=1), wd_ref[first * chunk:(j + 1) * chunk, :],
                           preferred_element_type=F32)
            acc = part if acc is None else acc + part
            acts = []
    out_ref[...] = h1 + acc


def _ffn(x, osb, olru, meta, wo, g2, wi, cw, cb, wd, bm, chunk):
    bsz, seq, d = x.shape
    row = pl.BlockSpec((None, bm, d), lambda b, i: (b, i, 0))

    def window(rows, width, first_row):
        return pl.BlockSpec((pl.Element(1), pl.Element(rows), pl.Element(width)),
                            lambda b, i: (b, pl.multiple_of(first_row(i), N_META), 0))

    mixer_specs = [window(bm, a.shape[-1], lambda i: N_META + i * bm) for a in (osb, olru)]
    meta_specs = [window(N_META, a.shape[-1], lambda i: 0) for a in (osb, olru)]
    return pl.pallas_call(
        functools.partial(_ffn_kernel, chunk=chunk, lookahead=3, down_group=6),
        grid=(bsz, seq // bm),
        in_specs=[row] + mixer_specs + [_const_spec(meta.shape)] + meta_specs
                 + [_const_spec(a.shape) for a in (wo, g2, wi, cw, cb, wd)],
        out_specs=row,
        out_shape=jax.ShapeDtypeStruct((bsz, seq, d), F32),
        scratch_shapes=[pltpu.VMEM((SUBLANES, wi.shape[1]), F32)],
        compiler_params=pltpu.CompilerParams(
            dimension_semantics=("arbitrary", "arbitrary"),
            vmem_limit_bytes=VMEM_LIMIT_BYTES),
        name="outproj_convffn",
    )(x, osb, olru, meta, osb, olru, wo, g2, wi, cw, cb, wd)


def _block_diag(w):
    n, c, d = w.shape
    eye = jnp.eye(n, dtype=w.dtype)
    return (eye[:, None, :, None] * w[:, :, None, :]).reshape(n * c, n * d)


def kernel(x, meta_tokens, norm1_g, w_in, q_norm_g, k_norm_g, conv_w, conv_b, w_rg_a, b_rg_a,
           w_rg_i, b_rg_i, lru_lambda, w_out, norm2_g, w_ffn_in, ffn_conv_w, ffn_conv_b, w_ffn_out):
    bsz, seq, d = x.shape
    t_len = -(-(seq + N_META) // Q_BLOCK) * Q_BLOCK
    meta = meta_tokens.astype(x.dtype)

    assert norm1_g.shape[0] == 1 and seq % Q_BLOCK == 0, (norm1_g.shape, seq)
    layer = 0
    lw = conv_w.shape[-1]
    sw = (w_in.shape[-1] - 2 * lw) // 3
    n_heads = sw // HEAD_DIM
    chunk = 2 * LANES
    head_id = jnp.arange(sw) // HEAD_DIM
    grp = jnp.where(head_id[:, None] == head_id[None, :], 1.0 / HEAD_DIM, 0.0).astype(BF16)
    row2d = lambda a: a.reshape(1, -1)

    w_gates = jnp.concatenate(
        [_block_diag(w_rg_a[layer]), _block_diag(w_rg_i[layer])], axis=1).astype(BF16)
    q, k, v, o_lru = _inproj(
        x, meta, row2d(norm1_g[layer]), w_in[layer].astype(BF16),
        row2d(jnp.tile(q_norm_g[layer], n_heads)), row2d(jnp.tile(k_norm_g[layer], n_heads)),
        grp, conv_w[layer], row2d(conv_b[layer]), w_gates,
        row2d(b_rg_a[layer]), row2d(b_rg_i[layer]), row2d(lru_lambda[layer]),
        t_len, _pick_block(t_len, 640))
    o_sb = _attention(q, k, v, Q_BLOCK)
    return _ffn(x, o_sb, o_lru, meta, w_out[layer].astype(BF16),
                row2d(norm2_g[layer]), w_ffn_in[layer].astype(BF16), ffn_conv_w[layer],
                row2d(ffn_conv_b[layer]), w_ffn_out[layer].astype(BF16),
                _pick_block(seq, 1024), chunk)
```
